```python
import math
import jax, jax.numpy as jnp
from jax import lax
import numpy as np

D_MODEL = 1024
BATCH = 16
SEQ = 2048
DEPTH = 1

GRID_W = 64
NA_HEADS = 8
NA_HEAD_DIM = 64
NA_WIN_ROWS_MAX = 8
NA_WIN_COLS = 16
DA_HEADS = 4
DA_HEAD_DIM = 64
NA_WIDTH = NA_HEADS * NA_HEAD_DIM
DA_WIDTH = DA_HEADS * 2 * DA_HEAD_DIM
MIX_WIDTH = NA_WIDTH + DA_WIDTH
IN_COLS = 3 * NA_WIDTH + 3 * DA_WIDTH
D_FF = -(-8 * D_MODEL // (3 * 256)) * 256
Q_BLOCK = 128
N_MOD = 6
RMS_EPS = 1e-6

kernel_name = "hybrid_na_diffattn_adaln_block"


def rms_norm(x, g):
    x32 = x.astype(jnp.float32)
    y = x32 * lax.rsqrt(jnp.mean(x32 * x32, axis=-1, keepdims=True) + RMS_EPS)
    return (y * g.astype(jnp.float32)).astype(x.dtype)


def neighbourhood_attention(q, k, v, rpb):
    B, S, H, d = q.shape
    rows = S // GRID_W
    wh = min(NA_WIN_ROWS_MAX, rows)
    ww = NA_WIN_COLS

    def to_grid(t):
        return t.reshape(B, rows, GRID_W, H, d).transpose(0, 3, 1, 2, 4)

    qg = to_grid(q * (d ** -0.5))
    kg = to_grid(k)
    vg = to_grid(v)
    col = jnp.arange(GRID_W)
    col_start = jnp.clip(col - ww // 2, 0, GRID_W - ww)
    in_win = (col[None, :] >= col_start[:, None]) & (col[None, :] < col_start[:, None] + ww)
    dc_idx = jnp.clip(col[None, :] - col[:, None] + ww - 1, 0, 2 * ww - 2)

    def row_step(args):
        r, q_r = args
        row_start = jnp.clip(r - wh // 2, 0, rows - wh)
        k_w = lax.dynamic_slice_in_dim(kg, row_start, wh, axis=2)
        v_w = lax.dynamic_slice_in_dim(vg, row_start, wh, axis=2)
        dr_idx = row_start + jnp.arange(wh) - r + NA_WIN_ROWS_MAX - 1
        bias = rpb[:, dr_idx[None, :, None], dc_idx[:, None, :]].astype(jnp.float32)
        s = jnp.einsum('bhqd,bhikd->bhqik', q_r, k_w).astype(jnp.float32) + bias[None]
        s = jnp.where(in_win[:, None, :], s, -jnp.inf)
        p = jax.nn.softmax(s.reshape(B, H, GRID_W, wh * GRID_W), axis=-1).reshape(s.shape)
        return jnp.einsum('bhqik,bhikd->bhqd', p.astype(v_w.dtype), v_w)

    out = lax.map(row_step, (jnp.arange(rows), qg.transpose(2, 0, 1, 3, 4)))
    return out.transpose(1, 0, 3, 2, 4).reshape(B, S, H * d)


def differential_attention(q, k, v, lam, slopes):
    B, S, H, _, d = q.shape
    nb = S // Q_BLOCK
    qh = (q * (d ** -0.5)).transpose(0, 2, 3, 1, 4)
    kh = k.transpose(0, 2, 3, 1, 4)
    vh = v.transpose(0, 2, 1, 3)
    q_blocks = qh.reshape(B, H, 2, nb, Q_BLOCK, d).transpose(3, 0, 1, 2, 4, 5)
    kpos = jnp.arange(S)

    def block_step(args):
        blk, qb = args
        qpos = blk * Q_BLOCK + jnp.arange(Q_BLOCK)
        dist = jnp.abs(qpos[:, None] - kpos[None, :]).astype(jnp.float32)
        s = jnp.einsum('bhmqd,bhmkd->bhmqk', qb, kh).astype(jnp.float32)
        s = s - slopes[:, None, None, None] * dist
        p = jax.nn.softmax(s, axis=-1)
        a = p[:, :, 0] - lam * p[:, :, 1]
        return jnp.einsum('bhqk,bhke->bhqe', a.astype(vh.dtype), vh)

    out = lax.map(block_step, (jnp.arange(nb), q_blocks))
    return out.transpose(1, 0, 3, 2, 4).reshape(B, S, H, 2 * d)


def setup_inputs(seed: int = 0) -> dict:
    key = jax.random.key(seed)
    ks = jax.random.split(key, 20)
    f32 = jnp.float32
    L, D = DEPTH, D_MODEL

    def nrm(k, shape, scale):
        return jax.random.normal(k, shape, f32) * scale

    return {
        "x": nrm(ks[0], (BATCH, SEQ, D), 1.0),
        "c": nrm(ks[1], (BATCH, D), 1.0),
        "w_ada": nrm(ks[2], (L, D, N_MOD * D), 0.5 * D ** -0.5),
        "b_ada": nrm(ks[3], (L, N_MOD * D), 0.02),
        "g_mix": 1.0 + nrm(ks[4], (L, D), 0.02),
        "w_in": nrm(ks[5], (L, D, IN_COLS), D ** -0.5),
        "rpb": nrm(ks[6], (L, NA_HEADS, 2 * NA_WIN_ROWS_MAX - 1, 2 * NA_WIN_COLS - 1), 0.2),
        "lambda_q1": nrm(ks[7], (L, DA_HEAD_DIM), 0.1),
        "lambda_k1": nrm(ks[8], (L, DA_HEAD_DIM), 0.1),
        "lambda_q2": nrm(ks[9], (L, DA_HEAD_DIM), 0.1),
        "lambda_k2": nrm(ks[10], (L, DA_HEAD_DIM), 0.1),
        "subln_w": 1.0 + nrm(ks[11], (L, 2 * DA_HEAD_DIM), 0.02),
        "w_out": nrm(ks[12], (L, MIX_WIDTH, D), MIX_WIDTH ** -0.5),
        "g_ffn": 1.0 + nrm(ks[13], (L, D), 0.02),
        "w_gate_up": nrm(ks[14], (L, D, 2 * D_FF), D ** -0.5),
        "w_down": nrm(ks[15], (L, D_FF, D), D_FF ** -0.5),
        "g_final": 1.0 + nrm(ks[16], (D,), 0.02),
    }


def reference(x, c, w_ada, b_ada, g_mix, w_in, rpb, lambda_q1, lambda_k1, lambda_q2,
              lambda_k2, subln_w, w_out, g_ffn, w_gate_up, w_down, g_final):
    B, S, D = x.shape
    f32 = jnp.float32
    slopes = 2.0 ** (-8.0 * jnp.arange(1, DA_HEADS + 1, dtype=f32) / DA_HEADS)
    split_pts = [NA_WIDTH, 2 * NA_WIDTH, 3 * NA_WIDTH,
                 3 * NA_WIDTH + DA_WIDTH, 3 * NA_WIDTH + 2 * DA_WIDTH]
    for layer in range(DEPTH):
        mod = jnp.einsum('bd,de->be', jax.nn.silu(c), w_ada[layer]) + b_ada[layer]
        sh_m, sc_m, gt_m, sh_f, sc_f, gt_f = [m[:, None, :] for m in jnp.split(mod, N_MOD, axis=-1)]

        h = rms_norm(x, g_mix[layer]) * (1.0 + sc_m) + sh_m
        proj = jnp.einsum('bsd,de->bse', h, w_in[layer])
        qa, ka, va, qb, kb, vb = jnp.split(proj, split_pts, axis=-1)

        o_a = neighbourhood_attention(
            qa.reshape(B, S, NA_HEADS, NA_HEAD_DIM),
            ka.reshape(B, S, NA_HEADS, NA_HEAD_DIM),
            va.reshape(B, S, NA_HEADS, NA_HEAD_DIM),
            rpb[layer])

        lambda_init = 0.8 - 0.6 * math.exp(-0.3 * layer)
        lam = (jnp.exp(jnp.sum(lambda_q1[layer].astype(f32) * lambda_k1[layer].astype(f32)))
               - jnp.exp(jnp.sum(lambda_q2[layer].astype(f32) * lambda_k2[layer].astype(f32)))
               + lambda_init)
        o_b = differential_attention(
            qb.reshape(B, S, DA_HEADS, 2, DA_HEAD_DIM),
            kb.reshape(B, S, DA_HEADS, 2, DA_HEAD_DIM),
            vb.reshape(B, S, DA_HEADS, 2 * DA_HEAD_DIM),
            lam, slopes)
        o_b = (rms_norm(o_b, subln_w[layer]) * (1.0 - lambda_init)).reshape(B, S, DA_WIDTH)

        mix = jnp.einsum('bse,ed->bsd', jnp.concatenate([o_a, o_b], axis=-1), w_out[layer])
        x = x + gt_m * mix

        h = rms_norm(x, g_ffn[layer]) * (1.0 + sc_f) + sh_f
        gate, up = jnp.split(jnp.einsum('bsd,df->bsf', h, w_gate_up[layer]), 2, axis=-1)
        ffn = jnp.einsum('bsf,fd->bsd', jax.nn.silu(gate) * up, w_down[layer])
        x = x + gt_f * ffn
    return rms_norm(x, g_final)
```

```python
import functools
import math

import jax
import jax.numpy as jnp
from jax import lax
from jax.experimental import pallas as pl
from jax.experimental.pallas import tpu as pltpu

F32 = jnp.float32
BF16 = jnp.bfloat16

GRID_W = 64
NA_HEADS = 8
NA_HEAD_DIM = 64
NA_WIN_ROWS = 8
NA_WIN_COLS = 16
DA_HEADS = 4
DA_HEAD_DIM = 64
N_MOD = 6
RMS_EPS = 1e-6
LAMBDA_INIT = 0.8 - 0.6 * math.exp(-0.3 * 0)

LANES = 128
NEG_BIG = -1e30
VMEM_LIMIT = 56 * 1024 * 1024

SH_M, SC_M, GT_M, SH_F, SC_F, GT_F = range(6)


def _rms(x):
    return x * lax.rsqrt(jnp.mean(x * x, axis=-1, keepdims=True) + RMS_EPS)


def _dot(a, b):
    return jnp.dot(a, b, preferred_element_type=F32)


def _dot_nt(a, b):
    return lax.dot_general(a, b, (((1,), (1,)), ((), ())), preferred_element_type=F32)


def _adaln_kernel(c_ref, w_ref, b_ref, o_ref):
    c = c_ref[...]
    s = c * jax.nn.sigmoid(c)
    o_ref[...] = jnp.dot(s, w_ref[...], preferred_element_type=F32,
                         precision=lax.Precision.HIGHEST) + b_ref[...]


def _adaln(c, w_ada, b_ada):
    B, D = c.shape
    E = w_ada.shape[1]
    bn = 1536
    return pl.pallas_call(
        _adaln_kernel,
        grid=(E // bn,),
        in_specs=[pl.BlockSpec((B, D), lambda j: (0, 0)),
                  pl.BlockSpec((D, bn), lambda j: (0, j)),
                  pl.BlockSpec((1, bn), lambda j: (0, j))],
        out_specs=pl.BlockSpec((B, bn), lambda j: (0, j)),
        out_shape=jax.ShapeDtypeStruct((B, E), F32),
        compiler_params=pltpu.CompilerParams(vmem_limit_bytes=VMEM_LIMIT),
        name="adaln",
    )(c, w_ada, b_ada.reshape(1, E))


def _in_proj_kernel(x_ref, mod_ref, g_ref, w_ref, o_ref):
    x = x_ref[0]
    sh = mod_ref[0, SH_M:SH_M + 1, :]
    sc = mod_ref[0, SC_M:SC_M + 1, :]
    h = _rms(x) * g_ref[...] * (1.0 + sc) + sh
    o_ref[0] = _dot(h.astype(BF16), w_ref[...]).astype(BF16)


def _in_proj(x, mod, g_mix, w_in_bf16, tm):
    B, S, D = x.shape
    E = w_in_bf16.shape[1]
    return pl.pallas_call(
        _in_proj_kernel,
        grid=(B, S // tm),
        in_specs=[pl.BlockSpec((1, tm, D), lambda b, i: (b, i, 0)),
                  pl.BlockSpec((1, N_MOD, D), lambda b, i: (b, 0, 0)),
                  pl.BlockSpec((1, D), lambda b, i: (0, 0)),
                  pl.BlockSpec((D, E), lambda b, i: (0, 0))],
        out_specs=pl.BlockSpec((1, tm, E), lambda b, i: (b, i, 0)),
        out_shape=jax.ShapeDtypeStruct((B, S, E), BF16),
        compiler_params=pltpu.CompilerParams(vmem_limit_bytes=VMEM_LIMIT),
        name="in_proj",
    )(x, mod, g_mix.reshape(1, D), w_in_bf16)


def _na_bias_kernel(rpb_ref, o_ref):
    h = pl.program_id(0)
    n_dr = 2 * NA_WIN_ROWS - 1
    n_dc = 2 * NA_WIN_COLS - 1
    q = lax.broadcasted_iota(jnp.int32, (GRID_W, LANES), 0)
    lane = lax.broadcasted_iota(jnp.int32, (GRID_W, LANES), 1)
    kc = lane & (GRID_W - 1)
    left = lane < GRID_W
    dc = kc - q + (NA_WIN_COLS - 1)
    col_start = jnp.clip(q - NA_WIN_COLS // 2, 0, GRID_W - NA_WIN_COLS)
    in_win = (kc >= col_start) & (kc < col_start + NA_WIN_COLS)
    base = h * (n_dr * n_dc)
    pair = []
    for dr in range(n_dr - 1):
        acc = jnp.zeros((GRID_W, LANES), F32)
        for d in range(n_dc):
            val = jnp.where(left, rpb_ref[base + dr * n_dc + d], rpb_ref[base + (dr + 1) * n_dc + d])
            acc = jnp.where(dc == d, val, acc)
        pair.append(jnp.where(in_win, acc, NEG_BIG))
    for off in range(NA_WIN_ROWS):
        for ip in range(NA_WIN_ROWS // 2):
            o_ref[0, off, :, ip * LANES:(ip + 1) * LANES] = pair[2 * ip - off + NA_WIN_ROWS - 1]


def _na_bias(rpb):
    H = rpb.shape[0]
    return pl.pallas_call(
        _na_bias_kernel,
        grid=(H,),
        in_specs=[pl.BlockSpec(memory_space=pltpu.SMEM)],
        out_specs=pl.BlockSpec((1, NA_WIN_ROWS, GRID_W, NA_WIN_ROWS * GRID_W), lambda h: (h, 0, 0, 0)),
        out_shape=jax.ShapeDtypeStruct((H, NA_WIN_ROWS, GRID_W, NA_WIN_ROWS * GRID_W), F32),
        name="na_bias",
    )(rpb.reshape(-1))


def _na_kernel(q_ref, k_ref, v_ref, bias_ref, o_ref, *, rows):
    lane = lax.broadcasted_iota(jnp.int32, (GRID_W, LANES), 1)
    left = lane < NA_HEAD_DIM
    win = NA_WIN_ROWS * GRID_W

    def row_step(r, carry):
        rs = jnp.clip(r - NA_WIN_ROWS // 2, 0, rows - NA_WIN_ROWS)
        off = r - rs
        q = q_ref[0, pl.ds(pl.multiple_of(r * GRID_W, GRID_W), GRID_W), :]
        k = k_ref[0, pl.ds(pl.multiple_of(rs * GRID_W, GRID_W), win), :]
        v = v_ref[0, pl.ds(pl.multiple_of(rs * GRID_W, GRID_W), win), :]
        outs = []
        for hh in range(2):
            qm = jnp.where(left if hh == 0 else jnp.logical_not(left), q, jnp.zeros_like(q))
            s = _dot_nt(qm, k) + bias_ref[hh, off]
            m = jnp.max(s, axis=-1, keepdims=True)
            p = jnp.exp(s - m)
            l = jnp.sum(p, axis=-1, keepdims=True)
            outs.append(_dot(p.astype(BF16), v) / l)
        o = jnp.where(left, outs[0], outs[1])
        o_ref[0, pl.ds(pl.multiple_of(r * GRID_W, GRID_W), GRID_W), :] = o.astype(BF16)
        return carry

    lax.fori_loop(0, rows, row_step, 0)


def _na_attn(proj, bias):
    B, S, _ = proj.shape
    rows = S // GRID_W
    n_pairs = NA_HEADS * NA_HEAD_DIM // LANES
    win = NA_WIN_ROWS * GRID_W
    return pl.pallas_call(
        functools.partial(_na_kernel, rows=rows),
        grid=(n_pairs, B),
        in_specs=[pl.BlockSpec((1, S, LANES), lambda hp, b: (b, 0, hp)),
                  pl.BlockSpec((1, S, LANES), lambda hp, b: (b, 0, n_pairs + hp)),
                  pl.BlockSpec((1, S, LANES), lambda hp, b: (b, 0, 2 * n_pairs + hp)),
                  pl.BlockSpec((2, NA_WIN_ROWS, GRID_W, win), lambda hp, b: (hp, 0, 0, 0))],
        out_specs=pl.BlockSpec((1, S, LANES), lambda hp, b: (b, 0, hp)),
        out_shape=jax.ShapeDtypeStruct((B, S, NA_HEADS * NA_HEAD_DIM), BF16),
        compiler_params=pltpu.CompilerParams(vmem_limit_bytes=VMEM_LIMIT),
        name="na_attn",
    )(proj, proj, proj, bias)


def _da_kernel(slopes_ref, q_ref, k_ref, v_ref, lam_ref, subln_ref, o_ref, *, qb):
    h = pl.program_id(1)
    i = pl.program_id(2)
    S = k_ref.shape[1]
    q = q_ref[0]
    k = k_ref[0]
    v = v_ref[0]
    lam_vecs = lam_ref[...]
    lam = (jnp.exp(jnp.sum(lam_vecs[0:1] * lam_vecs[1:2], axis=-1, keepdims=True))
           - jnp.exp(jnp.sum(lam_vecs[2:3] * lam_vecs[3:4], axis=-1, keepdims=True))
           + LAMBDA_INIT)
    lane = lax.broadcasted_iota(jnp.int32, (qb, LANES), 1)
    first = lane < DA_HEAD_DIM
    qpos = lax.broadcasted_iota(jnp.int32, (qb, S), 0) + i * qb
    kpos = lax.broadcasted_iota(jnp.int32, (qb, S), 1)
    bias = slopes_ref[h] * jnp.abs(qpos - kpos).astype(F32)
    outs = []
    for mp in range(2):
        qm = jnp.where(first if mp == 0 else jnp.logical_not(first), q, jnp.zeros_like(q))
        s = _dot_nt(qm, k) - bias
        m = jnp.max(s, axis=-1, keepdims=True)
        p = jnp.exp(s - m)
        l = jnp.sum(p, axis=-1, keepdims=True)
        outs.append(_dot(p.astype(BF16), v) / l)
    o = outs[0] - lam * outs[1]
    o_ref[0] = (_rms(o) * subln_ref[...] * (1.0 - LAMBDA_INIT)).astype(BF16)


def _da_attn(proj, slopes, lam_vecs, subln_w, qb):
    B, S, _ = proj.shape
    col0 = 3 * NA_HEADS * NA_HEAD_DIM // LANES
    return pl.pallas_call(
        functools.partial(_da_kernel, qb=qb),
        grid=(B, DA_HEADS, S // qb),
        in_specs=[pl.BlockSpec(memory_space=pltpu.SMEM),
                  pl.BlockSpec((1, qb, LANES), lambda b, h, i: (b, i, col0 + h)),
                  pl.BlockSpec((1, S, LANES), lambda b, h, i: (b, 0, col0 + DA_HEADS + h)),
                  pl.BlockSpec((1, S, LANES), lambda b, h, i: (b, 0, col0 + 2 * DA_HEADS + h)),
                  pl.BlockSpec((4, DA_HEAD_DIM), lambda b, h, i: (0, 0)),
                  pl.BlockSpec((1, 2 * DA_HEAD_DIM), lambda b, h, i: (0, 0))],
        out_specs=pl.BlockSpec((1, qb, LANES), lambda b, h, i: (b, i, h)),
        out_shape=jax.ShapeDtypeStruct((B, S, DA_HEADS * 2 * DA_HEAD_DIM), BF16),
        compiler_params=pltpu.CompilerParams(vmem_limit_bytes=VMEM_LIMIT),
        name="da_attn",
    )(slopes, proj, proj, proj, lam_vecs, subln_w.reshape(1, -1))


def _ffn_chunks(d_ff):
    chunks, start = [], 0
    while start < d_ff:
        size = min(1024, d_ff - start)
        chunks.append((start, size))
        start += size
    return chunks


def _out_ffn_kernel(oa_ref, ob_ref, x_ref, mod_ref, gffn_ref, gfin_ref, wout_ref, wgu_ref, wdown_ref,
                    o_ref, *, d_ff):
    na_w = oa_ref.shape[2]
    mix = _dot(oa_ref[0], wout_ref[0:na_w, :]) + _dot(ob_ref[0], wout_ref[na_w:, :])
    x1 = x_ref[0] + mod_ref[0, GT_M:GT_M + 1, :] * mix
    h = _rms(x1) * gffn_ref[...] * (1.0 + mod_ref[0, SC_F:SC_F + 1, :]) + mod_ref[0, SH_F:SH_F + 1, :]
    h = h.astype(BF16)
    ffn = jnp.zeros_like(x1)
    for start, size in _ffn_chunks(d_ff):
        gate = _dot(h, wgu_ref[:, start:start + size])
        up = _dot(h, wgu_ref[:, d_ff + start:d_ff + start + size])
        act = (gate * jax.nn.sigmoid(gate) * up).astype(BF16)
        ffn = ffn + _dot(act, wdown_ref[start:start + size, :])
    x2 = x1 + mod_ref[0, GT_F:GT_F + 1, :] * ffn
    o_ref[0] = _rms(x2) * gfin_ref[...]


def _out_ffn(o_a, o_b, x, mod, g_ffn, g_final, w_out, w_gate_up, w_down, tm):
    B, S, D = x.shape
    d_ff = w_down.shape[0]
    resident = pl.Buffered(1)
    return pl.pallas_call(
        functools.partial(_out_ffn_kernel, d_ff=d_ff),
        grid=(B, S // tm),
        in_specs=[pl.BlockSpec((1, tm, o_a.shape[2]), lambda b, i: (b, i, 0)),
                  pl.BlockSpec((1, tm, o_b.shape[2]), lambda b, i: (b, i, 0)),
                  pl.BlockSpec((1, tm, D), lambda b, i: (b, i, 0)),
                  pl.BlockSpec((1, N_MOD, D), lambda b, i: (b, 0, 0)),
                  pl.BlockSpec((1, D), lambda b, i: (0, 0)),
                  pl.BlockSpec((1, D), lambda b, i: (0, 0)),
                  pl.BlockSpec(w_out.shape, lambda b, i: (0, 0), pipeline_mode=resident),
                  pl.BlockSpec(w_gate_up.shape, lambda b, i: (0, 0), pipeline_mode=resident),
                  pl.BlockSpec(w_down.shape, lambda b, i: (0, 0), pipeline_mode=resident)],
        out_specs=pl.BlockSpec((1, tm, D), lambda b, i: (b, i, 0)),
        out_shape=jax.ShapeDtypeStruct((B, S, D), F32),
        compiler_params=pltpu.CompilerParams(vmem_limit_bytes=VMEM_LIMIT),
        name="out_ffn",
    )(o_a, o_b, x, mod, g_ffn.reshape(1, D), g_final.reshape(1, D), w_out, w_gate_up, w_down)


def kernel(x, c, w_ada, b_ada, g_mix, w_in, rpb, lambda_q1, lambda_k1, lambda_q2, lambda_k2, subln_w,
           w_out, g_ffn, w_gate_up, w_down, g_final):
    B, S, D = x.shape
    assert w_ada.shape[0] == 1, "single-layer block"
    na_w = NA_HEADS * NA_HEAD_DIM
    da_w = DA_HEADS * 2 * DA_HEAD_DIM

    col = jnp.arange(w_in.shape[2])
    is_q = (col < na_w) | ((col >= 3 * na_w) & (col < 3 * na_w + da_w))
    w_in_bf16 = (w_in[0] * jnp.where(is_q, NA_HEAD_DIM ** -0.5, 1.0)).astype(BF16)
    slopes = 2.0 ** (-8.0 * jnp.arange(1, DA_HEADS + 1, dtype=F32) / DA_HEADS)
    lam_vecs = jnp.stack([lambda_q1[0], lambda_k1[0], lambda_q2[0], lambda_k2[0]]).astype(F32)

    mod = _adaln(c, w_ada[0], b_ada[0]).reshape(B, N_MOD, D)
    proj = _in_proj(x, mod, g_mix[0], w_in_bf16, tm=512)
    o_a = _na_attn(proj, _na_bias(rpb[0]))
    o_b = _da_attn(proj, slopes, lam_vecs, subln_w[0], qb=256)
    return _out_ffn(o_a, o_b, x, mod, g_ffn[0], g_final, w_out[0].astype(BF16),
                    w_gate_up[0].astype(BF16), w_down[0].astype(BF16), tm=512)
```

```python
import functools
import math

import jax
import jax.numpy as jnp
from jax import lax
from jax.experimental import pallas as pl
from jax.experimental.pallas import tpu as pltpu

F32 = jnp.float32
BF16 = jnp.bfloat16

GRID_W = 64
NA_HEADS = 8
NA_HEAD_DIM = 64
NA_WIN_ROWS = 8
NA_WIN_COLS = 16
NA_Q_ROWS = 4
NA_K_ROWS = NA_Q_ROWS + NA_WIN_ROWS
DA_HEADS = 4
DA_HEAD_DIM = 64
N_MOD = 6
RMS_EPS = 1e-6
LAMBDA_INIT = 0.8 - 0.6 * math.exp(-0.3 * 0)

LANES = 128
NEG_BIG = -1e30
VMEM_LIMIT = 56 * 1024 * 1024

SH_M, SC_M, GT_M, SH_F, SC_F, GT_F = range(6)


def _rms(x):
    return x * lax.rsqrt(jnp.mean(x * x, axis=-1, keepdims=True) + RMS_EPS)


def _dot(a, b):
    return jnp.dot(a, b, preferred_element_type=F32)


def _dot_nt(a, b):
    return lax.dot_general(a, b, (((1,), (1,)), ((), ())), preferred_element_type=F32)


def _adaln_kernel(c_ref, w_ref, b_ref, o_ref):
    c = c_ref[...]
    s = c * jax.nn.sigmoid(c)
    o_ref[...] = jnp.dot(s, w_ref[...], preferred_element_type=F32,
                         precision=lax.Precision.HIGHEST) + b_ref[...]


def _adaln(c, w_ada, b_ada):
    B, D = c.shape
    E = w_ada.shape[1]
    bn = 1536
    return pl.pallas_call(
        _adaln_kernel,
        grid=(E // bn,),
        in_specs=[pl.BlockSpec((B, D), lambda j: (0, 0)),
                  pl.BlockSpec((D, bn), lambda j: (0, j)),
                  pl.BlockSpec((1, bn), lambda j: (0, j))],
        out_specs=pl.BlockSpec((B, bn), lambda j: (0, j)),
        out_shape=jax.ShapeDtypeStruct((B, E), F32),
        compiler_params=pltpu.CompilerParams(vmem_limit_bytes=VMEM_LIMIT),
        name="adaln",
    )(c, w_ada, b_ada.reshape(1, E))


def _in_proj_kernel(x_ref, mod_ref, g_ref, w_ref, o_ref):
    x = x_ref[0]
    sh = mod_ref[0, SH_M:SH_M + 1, :]
    sc = mod_ref[0, SC_M:SC_M + 1, :]
    h = _rms(x) * g_ref[...] * (1.0 + sc) + sh
    o_ref[0] = _dot(h.astype(BF16), w_ref[...]).astype(BF16)


def _in_proj(x, mod, g_mix, w_in_bf16, tm):
    B, S, D = x.shape
    E = w_in_bf16.shape[1]
    return pl.pallas_call(
        _in_proj_kernel,
        grid=(B, S // tm),
        in_specs=[pl.BlockSpec((1, tm, D), lambda b, i: (b, i, 0)),
                  pl.BlockSpec((1, N_MOD, D), lambda b, i: (b, 0, 0)),
                  pl.BlockSpec((1, D), lambda b, i: (0, 0)),
                  pl.BlockSpec((D, E), lambda b, i: (0, 0))],
        out_specs=pl.BlockSpec((1, tm, E), lambda b, i: (b, i, 0)),
        out_shape=jax.ShapeDtypeStruct((B, S, E), BF16),
        compiler_params=pltpu.CompilerParams(vmem_limit_bytes=VMEM_LIMIT),
        name="in_proj",
    )(x, mod, g_mix.reshape(1, D), w_in_bf16)


NA_TOP, NA_INTERIOR, NA_BOTTOM = range(3)


def _na_rel_row(variant, a, krl):
    if variant == NA_TOP:
        return krl - a + NA_WIN_ROWS - 1 if krl < NA_WIN_ROWS else None
    if variant == NA_INTERIOR:
        return krl - a + NA_WIN_ROWS // 2 - 1 if a <= krl < a + NA_WIN_ROWS else None
    return krl - a - 1 if krl >= NA_Q_ROWS else None


def _na_bias_kernel(rpb_ref, o_ref):
    h = pl.program_id(0)
    n_dr = 2 * NA_WIN_ROWS - 1
    n_dc = 2 * NA_WIN_COLS - 1
    q = lax.broadcasted_iota(jnp.int32, (GRID_W, LANES), 0)
    lane = lax.broadcasted_iota(jnp.int32, (GRID_W, LANES), 1)
    kc = lane & (GRID_W - 1)
    left = lane < GRID_W
    dc = kc - q + (NA_WIN_COLS - 1)
    col_start = jnp.clip(q - NA_WIN_COLS // 2, 0, GRID_W - NA_WIN_COLS)
    in_win = (kc >= col_start) & (kc < col_start + NA_WIN_COLS)
    base = h * (n_dr * n_dc)
    neg = jnp.full((GRID_W, LANES), NEG_BIG, F32)
    toeplitz = []
    for dr in range(n_dr):
        acc = jnp.zeros((GRID_W, LANES), F32)
        for d in range(n_dc):
            acc = jnp.where(dc == d, rpb_ref[base + dr * n_dc + d], acc)
        toeplitz.append(jnp.where(in_win, acc, NEG_BIG))
    for variant in range(3):
        for a in range(NA_Q_ROWS):
            for pair in range(NA_K_ROWS // 2):
                dr_l = _na_rel_row(variant, a, 2 * pair)
                dr_r = _na_rel_row(variant, a, 2 * pair + 1)
                t_l = neg if dr_l is None else toeplitz[dr_l]
                t_r = neg if dr_r is None else toeplitz[dr_r]
                tile = t_l if (dr_l is None and dr_r is None) else jnp.where(left, t_l, t_r)
                o_ref[0, variant, a * GRID_W:(a + 1) * GRID_W, pair * LANES:(pair + 1) * LANES] = tile


def _na_bias(rpb):
    H = rpb.shape[0]
    shape = (H, 3, NA_Q_ROWS * GRID_W, NA_K_ROWS * GRID_W)
    return pl.pallas_call(
        _na_bias_kernel,
        grid=(H,),
        in_specs=[pl.BlockSpec(memory_space=pltpu.SMEM)],
        out_specs=pl.BlockSpec((1,) + shape[1:], lambda h: (h, 0, 0, 0)),
        out_shape=jax.ShapeDtypeStruct(shape, F32),
        name="na_bias",
    )(rpb.reshape(-1))


def _na_kernel(q_ref, k_ref, v_ref, bias_ref, o_ref, *, rows):
    left = lax.broadcasted_iota(jnp.int32, (1, LANES), 1) < NA_HEAD_DIM
    n_blocks = rows // NA_Q_ROWS
    for bi in range(n_blocks):
        r0 = bi * NA_Q_ROWS
        ks = min(max(r0 - NA_WIN_ROWS // 2, 0), rows - NA_K_ROWS)
        variant = NA_TOP if bi == 0 else (NA_BOTTOM if bi == n_blocks - 1 else NA_INTERIOR)
        q = q_ref[0, r0 * GRID_W:(r0 + NA_Q_ROWS) * GRID_W, :]
        k = k_ref[0, ks * GRID_W:(ks + NA_K_ROWS) * GRID_W, :]
        v = v_ref[0, ks * GRID_W:(ks + NA_K_ROWS) * GRID_W, :]
        outs = []
        for hh in range(2):
            mine = left if hh == 0 else jnp.logical_not(left)
            qm = jnp.where(mine, q, jnp.zeros_like(q))
            vm = jnp.where(mine, v, jnp.ones_like(v))
            s = _dot_nt(qm, k) + bias_ref[hh, variant]
            m = jnp.max(s, axis=-1, keepdims=True)
            p = jnp.exp(s - m).astype(BF16)
            res = _dot(p, vm)
            outs.append(res / pltpu.roll(res, NA_HEAD_DIM, axis=1))
        o = jnp.where(left, outs[0], outs[1])
        o_ref[0, r0 * GRID_W:(r0 + NA_Q_ROWS) * GRID_W, :] = o.astype(BF16)


def _na_attn(proj, bias):
    B, S, _ = proj.shape
    rows = S // GRID_W
    assert rows % NA_Q_ROWS == 0 and rows >= 2 * NA_K_ROWS - NA_WIN_ROWS
    n_pairs = NA_HEADS * NA_HEAD_DIM // LANES
    return pl.pallas_call(
        functools.partial(_na_kernel, rows=rows),
        grid=(n_pairs, B),
        in_specs=[pl.BlockSpec((1, S, LANES), lambda hp, b: (b, 0, hp)),
                  pl.BlockSpec((1, S, LANES), lambda hp, b: (b, 0, n_pairs + hp)),
                  pl.BlockSpec((1, S, LANES), lambda hp, b: (b, 0, 2 * n_pairs + hp)),
                  pl.BlockSpec((2,) + bias.shape[1:], lambda hp, b: (hp, 0, 0, 0))],
        out_specs=pl.BlockSpec((1, S, LANES), lambda hp, b: (b, 0, hp)),
        out_shape=jax.ShapeDtypeStruct((B, S, NA_HEADS * NA_HEAD_DIM), BF16),
        compiler_params=pltpu.CompilerParams(vmem_limit_bytes=VMEM_LIMIT),
        name="na_attn",
    )(proj, proj, proj, bias)


def _da_kernel(slopes_ref, q_ref, k_ref, v_ref, lam_ref, subln_ref, o_ref, *, qb):
    h = pl.program_id(1)
    i = pl.program_id(2)
    S = k_ref.shape[1]
    q = q_ref[0]
    k = k_ref[0]
    v = v_ref[0]
    lam_vecs = lam_ref[...]
    lam = (jnp.exp(jnp.sum(lam_vecs[0:1] * lam_vecs[1:2], axis=-1, keepdims=True))
           - jnp.exp(jnp.sum(lam_vecs[2:3] * lam_vecs[3:4], axis=-1, keepdims=True))
           + LAMBDA_INIT)
    lane = lax.broadcasted_iota(jnp.int32, (qb, LANES), 1)
    first = lane < DA_HEAD_DIM
    qpos = lax.broadcasted_iota(jnp.int32, (qb, S), 0) + i * qb
    kpos = lax.broadcasted_iota(jnp.int32, (qb, S), 1)
    bias = slopes_ref[h] * jnp.abs(qpos - kpos).astype(F32)
    outs = []
    for mp in range(2):
        qm = jnp.where(first if mp == 0 else jnp.logical_not(first), q, jnp.zeros_like(q))
        s = _dot_nt(qm, k) - bias
        m = jnp.max(s, axis=-1, keepdims=True)
        p = jnp.exp(s - m)
        l = jnp.sum(p, axis=-1, keepdims=True)
        outs.append(_dot(p.astype(BF16), v) / l)
    o = outs[0] - lam * outs[1]
    o_ref[0] = (_rms(o) * subln_ref[...] * (1.0 - LAMBDA_INIT)).astype(BF16)


def _da_attn(proj, slopes, lam_vecs, subln_w, qb):
    B, S, _ = proj.shape
    col0 = 3 * NA_HEADS * NA_HEAD_DIM // LANES
    return pl.pallas_call(
        functools.partial(_da_kernel, qb=qb),
        grid=(B, DA_HEADS, S // qb),
        in_specs=[pl.BlockSpec(memory_space=pltpu.SMEM),
                  pl.BlockSpec((1, qb, LANES), lambda b, h, i: (b, i, col0 + h)),
                  pl.BlockSpec((1, S, LANES), lambda b, h, i: (b, 0, col0 + DA_HEADS + h)),
                  pl.BlockSpec((1, S, LANES), lambda b, h, i: (b, 0, col0 + 2 * DA_HEADS + h)),
                  pl.BlockSpec((4, DA_HEAD_DIM), lambda b, h, i: (0, 0)),
                  pl.BlockSpec((1, 2 * DA_HEAD_DIM), lambda b, h, i: (0, 0))],
        out_specs=pl.BlockSpec((1, qb, LANES), lambda b, h, i: (b, i, h)),
        out_shape=jax.ShapeDtypeStruct((B, S, DA_HEADS * 2 * DA_HEAD_DIM), BF16),
        compiler_params=pltpu.CompilerParams(vmem_limit_bytes=VMEM_LIMIT),
        name="da_attn",
    )(slopes, proj, proj, proj, lam_vecs, subln_w.reshape(1, -1))


def _ffn_chunks(d_ff):
    chunks, start = [], 0
    while start < d_ff:
        size = min(1024, d_ff - start)
        chunks.append((start, size))
        start += size
    return chunks


def _out_ffn_kernel(oa_ref, ob_ref, x_ref, mod_ref, gffn_ref, gfin_ref, wout_ref, wgu_ref, wdown_ref,
                    o_ref, *, d_ff):
    na_w = oa_ref.shape[2]
    mix = _dot(oa_ref[0], wout_ref[0:na_w, :]) + _dot(ob_ref[0], wout_ref[na_w:, :])
    x1 = x_ref[0] + mod_ref[0, GT_M:GT_M + 1, :] * mix
    h = _rms(x1) * gffn_ref[...] * (1.0 + mod_ref[0, SC_F:SC_F + 1, :]) + mod_ref[0, SH_F:SH_F + 1, :]
    h = h.astype(BF16)
    ffn = jnp.zeros_like(x1)
    for start, size in _ffn_chunks(d_ff):
        gate = _dot(h, wgu_ref[:, start:start + size])
        up = _dot(h, wgu_ref[:, d_ff + start:d_ff + start + size])
        act = (gate * jax.nn.sigmoid(gate) * up).astype(BF16)
        ffn = ffn + _dot(act, wdown_ref[start:start + size, :])
    x2 = x1 + mod_ref[0, GT_F:GT_F + 1, :] * ffn
    o_ref[0] = _rms(x2) * gfin_ref[...]


def _out_ffn(o_a, o_b, x, mod, g_ffn, g_final, w_out, w_gate_up, w_down, tm):
    B, S, D = x.shape
    d_ff = w_down.shape[0]
    resident = pl.Buffered(1)
    return pl.pallas_call(
        functools.partial(_out_ffn_kernel, d_ff=d_ff),
        grid=(B, S // tm),
        in_specs=[pl.BlockSpec((1, tm, o_a.shape[2]), lambda b, i: (b, i, 0)),
                  pl.BlockSpec((1, tm, o_b.shape[2]), lambda b, i: (b, i, 0)),
                  pl.BlockSpec((1, tm, D), lambda b, i: (b, i, 0)),
                  pl.BlockSpec((1, N_MOD, D), lambda b, i: (b, 0, 0)),
                  pl.BlockSpec((1, D), lambda b, i: (0, 0)),
                  pl.BlockSpec((1, D), lambda b, i: (0, 0)),
                  pl.BlockSpec(w_out.shape, lambda b, i: (0, 0), pipeline_mode=resident),
                  pl.BlockSpec(w_gate_up.shape, lambda b, i: (0, 0), pipeline_mode=resident),
                  pl.BlockSpec(w_down.shape, lambda b, i: (0, 0), pipeline_mode=resident)],
        out_specs=pl.BlockSpec((1, tm, D), lambda b, i: (b, i, 0)),
        out_shape=jax.ShapeDtypeStruct((B, S, D), F32),
        compiler_params=pltpu.CompilerParams(vmem_limit_bytes=VMEM_LIMIT),
        name="out_ffn",
    )(o_a, o_b, x, mod, g_ffn.reshape(1, D), g_final.reshape(1, D), w_out, w_gate_up, w_down)


def kernel(x, c, w_ada, b_ada, g_mix, w_in, rpb, lambda_q1, lambda_k1, lambda_q2, lambda_k2, subln_w,
           w_out, g_ffn, w_gate_up, w_down, g_final):
    B, S, D = x.shape
    assert w_ada.shape[0] == 1, "single-layer block"
    na_w = NA_HEADS * NA_HEAD_DIM
    da_w = DA_HEADS * 2 * DA_HEAD_DIM

    col = jnp.arange(w_in.shape[2])
    is_q = (col < na_w) | ((col >= 3 * na_w) & (col < 3 * na_w + da_w))
    w_in_bf16 = (w_in[0] * jnp.where(is_q, NA_HEAD_DIM ** -0.5, 1.0)).astype(BF16)
    slopes = 2.0 ** (-8.0 * jnp.arange(1, DA_HEADS + 1, dtype=F32) / DA_HEADS)
    lam_vecs = jnp.stack([lambda_q1[0], lambda_k1[0], lambda_q2[0], lambda_k2[0]]).astype(F32)

    mod = _adaln(c, w_ada[0], b_ada[0]).reshape(B, N_MOD, D)
    proj = _in_proj(x, mod, g_mix[0], w_in_bf16, tm=512)
    o_a = _na_attn(proj, _na_bias(rpb[0]))
    o_b = _da_attn(proj, slopes, lam_vecs, subln_w[0], qb=256)
    return _out_ffn(o_a, o_b, x, mod, g_ffn[0], g_final, w_out[0].astype(BF16),
                    w_gate_up[0].astype(BF16), w_down[0].astype(BF16), tm=512)
```

```python
import functools
import math

import jax
import jax.numpy as jnp
import numpy as np
from jax import lax
from jax.experimental import pallas as pl
from jax.experimental.pallas import tpu as pltpu

F32 = jnp.float32
BF16 = jnp.bfloat16

GRID_W = 64
NA_HEADS = 8
NA_HEAD_DIM = 64
NA_WIN_ROWS = 8
NA_WIN_COLS = 16
NA_Q_ROWS = 4
NA_K_ROWS = NA_Q_ROWS + NA_WIN_ROWS
DA_HEADS = 4
DA_HEAD_DIM = 64
N_MOD = 6
RMS_EPS = 1e-6
LAMBDA_INIT = 0.8 - 0.6 * math.exp(-0.3 * 0)

LANES = 128
NEG_BIG = -1e30
VMEM_LIMIT = 56 * 1024 * 1024

SH_M, SC_M, GT_M, SH_F, SC_F, GT_F = range(6)


def _rms(x):
    return x * lax.rsqrt(jnp.mean(x * x, axis=-1, keepdims=True) + RMS_EPS)


def _dot(a, b):
    return jnp.dot(a, b, preferred_element_type=F32)


def _dot_nt(a, b):
    return lax.dot_general(a, b, (((1,), (1,)), ((), ())), preferred_element_type=F32)


def _adaln_kernel(c_ref, w_ref, b_ref, o_ref):
    c = c_ref[...]
    s = c * jax.nn.sigmoid(c)
    o_ref[...] = jnp.dot(s, w_ref[...], preferred_element_type=F32,
                         precision=lax.Precision.HIGHEST) + b_ref[...]


def _adaln(c, w_ada, b_ada):
    B, D = c.shape
    E = w_ada.shape[1]
    bn = 1536
    return pl.pallas_call(
        _adaln_kernel,
        grid=(E // bn,),
        in_specs=[pl.BlockSpec((B, D), lambda j: (0, 0)),
                  pl.BlockSpec((D, bn), lambda j: (0, j)),
                  pl.BlockSpec((1, bn), lambda j: (0, j))],
        out_specs=pl.BlockSpec((B, bn), lambda j: (0, j)),
        out_shape=jax.ShapeDtypeStruct((B, E), F32),
        compiler_params=pltpu.CompilerParams(vmem_limit_bytes=VMEM_LIMIT),
        name="adaln",
    )(c, w_ada, b_ada.reshape(1, E))


def _in_proj_kernel(x_ref, mod_ref, g_ref, w_ref, o_ref):
    x = x_ref[0]
    sh = mod_ref[0, SH_M:SH_M + 1, :]
    sc = mod_ref[0, SC_M:SC_M + 1, :]
    h = _rms(x) * g_ref[...] * (1.0 + sc) + sh
    o_ref[0] = _dot(h.astype(BF16), w_ref[...]).astype(BF16)


def _in_proj(x, mod, g_mix, w_in_bf16, tm):
    B, S, D = x.shape
    E = w_in_bf16.shape[1]
    return pl.pallas_call(
        _in_proj_kernel,
        grid=(B, S // tm),
        in_specs=[pl.BlockSpec((1, tm, D), lambda b, i: (b, i, 0)),
                  pl.BlockSpec((1, N_MOD, D), lambda b, i: (b, 0, 0)),
                  pl.BlockSpec((1, D), lambda b, i: (0, 0)),
                  pl.BlockSpec((D, E), lambda b, i: (0, 0))],
        out_specs=pl.BlockSpec((1, tm, E), lambda b, i: (b, i, 0)),
        out_shape=jax.ShapeDtypeStruct((B, S, E), BF16),
        compiler_params=pltpu.CompilerParams(vmem_limit_bytes=VMEM_LIMIT),
        name="in_proj",
    )(x, mod, g_mix.reshape(1, D), w_in_bf16)


NA_TOP, NA_INTERIOR, NA_BOTTOM = range(3)


def _na_rel_row(variant, a, krl):
    if variant == NA_TOP:
        return krl - a + NA_WIN_ROWS - 1 if krl < NA_WIN_ROWS else None
    if variant == NA_INTERIOR:
        return krl - a + NA_WIN_ROWS // 2 - 1 if a <= krl < a + NA_WIN_ROWS else None
    return krl - a - 1 if krl >= NA_Q_ROWS else None


def _na_bias_kernel(rpb_ref, o_ref):
    h = pl.program_id(0)
    n_dr = 2 * NA_WIN_ROWS - 1
    n_dc = 2 * NA_WIN_COLS - 1
    q = lax.broadcasted_iota(jnp.int32, (GRID_W, LANES), 0)
    lane = lax.broadcasted_iota(jnp.int32, (GRID_W, LANES), 1)
    kc = lane & (GRID_W - 1)
    left = lane < GRID_W
    dc = kc - q + (NA_WIN_COLS - 1)
    col_start = jnp.clip(q - NA_WIN_COLS // 2, 0, GRID_W - NA_WIN_COLS)
    in_win = (kc >= col_start) & (kc < col_start + NA_WIN_COLS)
    base = h * (n_dr * n_dc)
    neg = jnp.full((GRID_W, LANES), NEG_BIG, F32)
    toeplitz = []
    for dr in range(n_dr):
        acc = jnp.zeros((GRID_W, LANES), F32)
        for d in range(n_dc):
            acc = jnp.where(dc == d, rpb_ref[base + dr * n_dc + d], acc)
        toeplitz.append(jnp.where(in_win, acc, NEG_BIG))
    for variant in range(3):
        for a in range(NA_Q_ROWS):
            for pair in range(NA_K_ROWS // 2):
                dr_l = _na_rel_row(variant, a, 2 * pair)
                dr_r = _na_rel_row(variant, a, 2 * pair + 1)
                t_l = neg if dr_l is None else toeplitz[dr_l]
                t_r = neg if dr_r is None else toeplitz[dr_r]
                tile = t_l if (dr_l is None and dr_r is None) else jnp.where(left, t_l, t_r)
                o_ref[0, variant, a * GRID_W:(a + 1) * GRID_W, pair * LANES:(pair + 1) * LANES] = tile


def _na_bias(rpb):
    H = rpb.shape[0]
    shape = (H, 3, NA_Q_ROWS * GRID_W, NA_K_ROWS * GRID_W)
    return pl.pallas_call(
        _na_bias_kernel,
        grid=(H,),
        in_specs=[pl.BlockSpec(memory_space=pltpu.SMEM)],
        out_specs=pl.BlockSpec((1,) + shape[1:], lambda h: (h, 0, 0, 0)),
        out_shape=jax.ShapeDtypeStruct(shape, F32),
        name="na_bias",
    )(rpb.reshape(-1))


def _na_kernel(q_ref, k_ref, v_ref, bias_ref, o_ref, *, rows):
    left = lax.broadcasted_iota(jnp.int32, (1, LANES), 1) < NA_HEAD_DIM
    n_blocks = rows // NA_Q_ROWS
    for bi in range(n_blocks):
        r0 = bi * NA_Q_ROWS
        ks = min(max(r0 - NA_WIN_ROWS // 2, 0), rows - NA_K_ROWS)
        variant = NA_TOP if bi == 0 else (NA_BOTTOM if bi == n_blocks - 1 else NA_INTERIOR)
        q = q_ref[0, r0 * GRID_W:(r0 + NA_Q_ROWS) * GRID_W, :]
        k = k_ref[0, ks * GRID_W:(ks + NA_K_ROWS) * GRID_W, :]
        v = v_ref[0, ks * GRID_W:(ks + NA_K_ROWS) * GRID_W, :]
        outs = []
        for hh in range(2):
            mine = left if hh == 0 else jnp.logical_not(left)
            qm = jnp.where(mine, q, jnp.zeros_like(q))
            vm = jnp.where(mine, v, jnp.ones_like(v))
            s = _dot_nt(qm, k) + bias_ref[hh, variant]
            m = jnp.max(s, axis=-1, keepdims=True)
            p = jnp.exp(s - m).astype(BF16)
            res = _dot(p, vm)
            outs.append(res / pltpu.roll(res, NA_HEAD_DIM, axis=1))
        o = jnp.where(left, outs[0], outs[1])
        o_ref[0, r0 * GRID_W:(r0 + NA_Q_ROWS) * GRID_W, :] = o.astype(BF16)


def _na_attn(proj, bias):
    B, S, _ = proj.shape
    rows = S // GRID_W
    assert rows % NA_Q_ROWS == 0 and rows >= 2 * NA_K_ROWS - NA_WIN_ROWS
    n_pairs = NA_HEADS * NA_HEAD_DIM // LANES
    return pl.pallas_call(
        functools.partial(_na_kernel, rows=rows),
        grid=(n_pairs, B),
        in_specs=[pl.BlockSpec((1, S, LANES), lambda hp, b: (b, 0, hp)),
                  pl.BlockSpec((1, S, LANES), lambda hp, b: (b, 0, n_pairs + hp)),
                  pl.BlockSpec((1, S, LANES), lambda hp, b: (b, 0, 2 * n_pairs + hp)),
                  pl.BlockSpec((2,) + bias.shape[1:], lambda hp, b: (hp, 0, 0, 0))],
        out_specs=pl.BlockSpec((1, S, LANES), lambda hp, b: (b, 0, hp)),
        out_shape=jax.ShapeDtypeStruct((B, S, NA_HEADS * NA_HEAD_DIM), BF16),
        compiler_params=pltpu.CompilerParams(vmem_limit_bytes=VMEM_LIMIT),
        name="na_attn",
    )(proj, proj, proj, bias)


DA_BLOCK = 512
DA_ONES_ROWS = 16
BF16_EXACT_INT = 256


def _da_tables(n_tiles):
    P = DA_BLOCK
    slopes = np.float32(2.0) ** (-8.0 * np.arange(1, DA_HEADS + 1, dtype=np.float32) / DA_HEADS)
    assert np.all(np.log2(slopes) == np.round(np.log2(slopes))), "feature exactness needs power-of-two slopes"
    pos = np.arange(P)
    lo, hi = pos % BF16_EXACT_INT, pos - pos % BF16_EXACT_INT
    d = np.arange(-(n_tiles - 1), n_tiles)
    sg = np.sign(d)
    kfeat = np.zeros((2 * n_tiles - 1, P, LANES), np.float32)
    kfeat[:, :, 0] = (-P * np.abs(d))[:, None]
    kfeat[:, :, 1] = -sg[:, None] * lo[None, :]
    kfeat[:, :, 2] = -sg[:, None] * hi[None, :]
    kfeat[:, :, 3] = sg[:, None]
    kfeat[:, :, 4] = sg[:, None]
    qfeat = np.zeros((DA_HEADS, P, LANES), np.float32)
    qfeat[:, :, 0:3] = slopes[:, None, None]
    qfeat[:, :, 3] = slopes[:, None] * lo[None, :]
    qfeat[:, :, 4] = slopes[:, None] * hi[None, :]
    diag = -slopes[:, None, None] * np.abs(pos[:, None] - pos[None, :]).astype(np.float32)
    return (jnp.asarray(kfeat.reshape(-1, LANES), BF16), jnp.asarray(qfeat, BF16), jnp.asarray(diag, F32))


def _da_kernel(q_ref, k_ref, v_ref, kfeat_ref, qfeat_ref, diag_ref, lam_ref, subln_ref, o_ref,
               vt_ref, s_ref, *, n_tiles):
    P = DA_BLOCK
    S = n_tiles * P
    dv = 2 * DA_HEAD_DIM
    n_slots = s_ref.shape[0]

    vt_ref[0:dv, :] = v_ref[0].astype(F32).T.astype(BF16)
    vt_ref[dv:, :] = jnp.ones((DA_ONES_ROWS, S), BF16)

    lam_vecs = lam_ref[...]
    lam = (jnp.exp(jnp.sum(lam_vecs[0:1] * lam_vecs[1:2], axis=-1, keepdims=True))
           - jnp.exp(jnp.sum(lam_vecs[2:3] * lam_vecs[3:4], axis=-1, keepdims=True))
           + LAMBDA_INIT)
    first = lax.broadcasted_iota(jnp.int32, (1, LANES), 1) < DA_HEAD_DIM
    qfeat = qfeat_ref[0]

    def scores(blk, mp, slot):
        q = q_ref[0, blk * P:(blk + 1) * P, :]
        qm = jnp.where(first if mp == 0 else jnp.logical_not(first), q, jnp.zeros_like(q))
        q_aug = jnp.concatenate([qm, qfeat], axis=1)
        m = None
        for t in range(n_tiles):
            feat0 = (n_tiles - 1 - blk + t) * P
            k_aug = jnp.concatenate([k_ref[0, t * P:(t + 1) * P, :], kfeat_ref[feat0:feat0 + P, :]], axis=1)
            st = _dot_nt(k_aug, q_aug)
            if t == blk:
                st = st + diag_ref[0]
            s_ref[slot, t] = st
            mt = jnp.max(st, axis=0, keepdims=True)
            m = mt if m is None else jnp.maximum(m, mt)
        return m

    def weighted_values(slot, m):
        acc = jnp.zeros((dv + DA_ONES_ROWS, P), F32)
        for t in range(n_tiles):
            e = jnp.exp(s_ref[slot, t] - m).astype(BF16)
            acc = acc + _dot(vt_ref[:, t * P:(t + 1) * P], e)
        return acc[0:dv] / acc[dv:dv + 1]

    chains = [(blk, mp) for blk in range(n_tiles) for mp in range(2)]
    maxes, outs = {}, {}
    for n in range(len(chains) + 1):
        if n < len(chains):
            maxes[n] = scores(*chains[n], n % n_slots)
        if n >= 1:
            blk, mp = chains[n - 1]
            outs[mp] = weighted_values((n - 1) % n_slots, maxes.pop(n - 1))
            if mp == 1:
                o = (outs[0] - lam * outs[1]).T
                o_ref[0, blk * P:(blk + 1) * P, :] = (
                    _rms(o) * subln_ref[...] * (1.0 - LAMBDA_INIT)).astype(BF16)


def _da_attn(proj, lam_vecs, subln_w):
    B, S, _ = proj.shape
    P = DA_BLOCK
    assert S % P == 0
    n_tiles = S // P
    kfeat, qfeat, diag = _da_tables(n_tiles)
    col0 = 3 * NA_HEADS * NA_HEAD_DIM // LANES
    dv = 2 * DA_HEAD_DIM
    n_slots = 3
    return pl.pallas_call(
        functools.partial(_da_kernel, n_tiles=n_tiles),
        grid=(B, DA_HEADS),
        in_specs=[pl.BlockSpec((1, S, LANES), lambda b, h: (b, 0, col0 + h)),
                  pl.BlockSpec((1, S, LANES), lambda b, h: (b, 0, col0 + DA_HEADS + h)),
                  pl.BlockSpec((1, S, LANES), lambda b, h: (b, 0, col0 + 2 * DA_HEADS + h)),
                  pl.BlockSpec(kfeat.shape, lambda b, h: (0, 0)),
                  pl.BlockSpec((1, P, LANES), lambda b, h: (h, 0, 0)),
                  pl.BlockSpec((1, P, P), lambda b, h: (h, 0, 0)),
                  pl.BlockSpec((4, DA_HEAD_DIM), lambda b, h: (0, 0)),
                  pl.BlockSpec((1, dv), lambda b, h: (0, 0))],
        out_specs=pl.BlockSpec((1, S, LANES), lambda b, h: (b, 0, h)),
        out_shape=jax.ShapeDtypeStruct((B, S, DA_HEADS * dv), BF16),
        scratch_shapes=[pltpu.VMEM((dv + DA_ONES_ROWS, S), BF16),
                        pltpu.VMEM((n_slots, n_tiles, P, P), F32)],
        compiler_params=pltpu.CompilerParams(vmem_limit_bytes=VMEM_LIMIT),
        name="da_attn",
    )(proj, proj, proj, kfeat, qfeat, diag, lam_vecs, subln_w.reshape(1, -1))


def _ffn_chunks(d_ff):
    chunks, start = [], 0
    while start < d_ff:
        size = min(1024, d_ff - start)
        chunks.append((start, size))
        start += size
    return chunks


def _out_ffn_kernel(oa_ref, ob_ref, x_ref, mod_ref, gffn_ref, gfin_ref, wout_ref, wgu_ref, wdown_ref,
                    o_ref, *, d_ff):
    na_w = oa_ref.shape[2]
    mix = _dot(oa_ref[0], wout_ref[0:na_w, :]) + _dot(ob_ref[0], wout_ref[na_w:, :])
    x1 = x_ref[0] + mod_ref[0, GT_M:GT_M + 1, :] * mix
    h = _rms(x1) * gffn_ref[...] * (1.0 + mod_ref[0, SC_F:SC_F + 1, :]) + mod_ref[0, SH_F:SH_F + 1, :]
    h = h.astype(BF16)
    ffn = jnp.zeros_like(x1)
    for start, size in _ffn_chunks(d_ff):
        gate = _dot(h, wgu_ref[:, start:start + size])
        up = _dot(h, wgu_ref[:, d_ff + start:d_ff + start + size])
        act = (gate * jax.nn.sigmoid(gate) * up).astype(BF16)
        ffn = ffn + _dot(act, wdown_ref[start:start + size, :])
    x2 = x1 + mod_ref[0, GT_F:GT_F + 1, :] * ffn
    o_ref[0] = _rms(x2) * gfin_ref[...]


def _out_ffn(o_a, o_b, x, mod, g_ffn, g_final, w_out, w_gate_up, w_down, tm):
    B, S, D = x.shape
    d_ff = w_down.shape[0]
    resident = pl.Buffered(1)
    return pl.pallas_call(
        functools.partial(_out_ffn_kernel, d_ff=d_ff),
        grid=(B, S // tm),
        in_specs=[pl.BlockSpec((1, tm, o_a.shape[2]), lambda b, i: (b, i, 0)),
                  pl.BlockSpec((1, tm, o_b.shape[2]), lambda b, i: (b, i, 0)),
                  pl.BlockSpec((1, tm, D), lambda b, i: (b, i, 0)),
                  pl.BlockSpec((1, N_MOD, D), lambda b, i: (b, 0, 0)),
                  pl.BlockSpec((1, D), lambda b, i: (0, 0)),
                  pl.BlockSpec((1, D), lambda b, i: (0, 0)),
                  pl.BlockSpec(w_out.shape, lambda b, i: (0, 0), pipeline_mode=resident),
                  pl.BlockSpec(w_gate_up.shape, lambda b, i: (0, 0), pipeline_mode=resident),
                  pl.BlockSpec(w_down.shape, lambda b, i: (0, 0), pipeline_mode=resident)],
        out_specs=pl.BlockSpec((1, tm, D), lambda b, i: (b, i, 0)),
        out_shape=jax.ShapeDtypeStruct((B, S, D), F32),
        compiler_params=pltpu.CompilerParams(vmem_limit_bytes=VMEM_LIMIT),
        name="out_ffn",
    )(o_a, o_b, x, mod, g_ffn.reshape(1, D), g_final.reshape(1, D), w_out, w_gate_up, w_down)


def kernel(x, c, w_ada, b_ada, g_mix, w_in, rpb, lambda_q1, lambda_k1, lambda_q2, lambda_k2, subln_w,
           w_out, g_ffn, w_gate_up, w_down, g_final):
    B, S, D = x.shape
    assert w_ada.shape[0] == 1, "single-layer block"
    na_w = NA_HEADS * NA_HEAD_DIM
    da_w = DA_HEADS * 2 * DA_HEAD_DIM

    col = jnp.arange(w_in.shape[2])
    is_q = (col < na_w) | ((col >= 3 * na_w) & (col < 3 * na_w + da_w))
    w_in_bf16 = (w_in[0] * jnp.where(is_q, NA_HEAD_DIM ** -0.5, 1.0)).astype(BF16)
    lam_vecs = jnp.stack([lambda_q1[0], lambda_k1[0], lambda_q2[0], lambda_k2[0]]).astype(F32)

    mod = _adaln(c, w_ada[0], b_ada[0]).reshape(B, N_MOD, D)
    proj = _in_proj(x, mod, g_mix[0], w_in_bf16, tm=512)
    o_a = _na_attn(proj, _na_bias(rpb[0]))
    o_b = _da_attn(proj, lam_vecs, subln_w[0])
    return _out_ffn(o_a, o_b, x, mod, g_ffn[0], g_final, w_out[0].astype(BF16),
                    w_gate_up[0].astype(BF16), w_down[0].astype(BF16), tm=512)
```
